```python
import math
import jax, jax.numpy as jnp
from jax import lax
import numpy as np

D_MODEL = 1024
BATCH = 8
SEQ = 2048
DEPTH = 1

CHUNK = 64
D_MIX = 2 * D_MODEL
D_SSD = D_MIX // 2
D_CONF = D_MIX - D_SSD
SSD_HEAD_DIM = 64
SSD_HEADS = D_SSD // SSD_HEAD_DIM
SSD_GROUPS = 2
SSD_STATE = 128
SSD_CONV = 4
CONF_CONV = 31
FFN_CONV = 3
D_FF = 2816
D_XBC = D_SSD + 2 * SSD_GROUPS * SSD_STATE
IN_Z = D_SSD
IN_XBC = IN_Z + D_XBC
IN_DT = IN_XBC + SSD_HEADS
D_IN = IN_DT + 2 * D_CONF
N_MOD = 6

kernel_name = "hymba_ssd_conformer_convffn_block"


def rmsnorm(x, w, eps=1e-6):
    xf = x.astype(jnp.float32)
    y = xf * lax.rsqrt(jnp.mean(xf * xf, axis=-1, keepdims=True) + eps)
    return (y * w.astype(jnp.float32)).astype(x.dtype)


def layernorm(x, w, b, eps=1e-5):
    xf = x.astype(jnp.float32)
    mu = jnp.mean(xf, axis=-1, keepdims=True)
    var = jnp.mean(jnp.square(xf - mu), axis=-1, keepdims=True)
    y = (xf - mu) * lax.rsqrt(var + eps)
    return (y * w.astype(jnp.float32) + b.astype(jnp.float32)).astype(x.dtype)


def causal_dwconv(x, w, b):
    k = w.shape[0]
    y = lax.conv_general_dilated(
        x, w[:, None, :].astype(x.dtype), window_strides=(1,),
        padding=[(k - 1, 0)], dimension_numbers=("NWC", "WIO", "NWC"),
        feature_group_count=x.shape[-1])
    return y + b


def segsum(a):
    t = a.shape[-1]
    cs = jnp.cumsum(a, axis=-1)
    diff = cs[..., :, None] - cs[..., None, :]
    mask = jnp.tril(jnp.ones((t, t), dtype=bool))
    return jnp.where(mask, diff, -jnp.inf)


def ssd_scan(x, dt, a, b_in, c_in):
    bsz, seq, nh, hp = x.shape
    g, n = b_in.shape[2], b_in.shape[3]
    r = nh // g
    nc = seq // CHUNK
    xdt = (x.astype(jnp.float32) * dt[..., None]).reshape(bsz, nc, CHUNK, g, r, hp)
    a_dt = (a * dt).reshape(bsz, nc, CHUNK, g, r)
    a_dt = jnp.transpose(a_dt, (0, 3, 4, 1, 2))
    bc = b_in.astype(jnp.float32).reshape(bsz, nc, CHUNK, g, n)
    cc = c_in.astype(jnp.float32).reshape(bsz, nc, CHUNK, g, n)
    a_cs = jnp.cumsum(a_dt, axis=-1)
    decay = jnp.exp(segsum(a_dt))
    cb = jnp.einsum("bclgn,bcsgn->bgcls", cc, bc)
    y_diag = jnp.einsum("bgcls,bgrcls,bcsgrp->bclgrp", cb, decay, xdt)
    decay_states = jnp.exp(a_cs[..., -1:] - a_cs)
    states = jnp.einsum("bclgn,bgrcl,bclgrp->bcgrpn", bc, decay_states, xdt)
    chunk_decay = jnp.exp(a_cs[..., -1])

    def step(h, inp):
        s, d = inp
        return h * d[..., None, None] + s, h

    h0 = jnp.zeros((bsz, g, r, hp, n), jnp.float32)
    _, prev = lax.scan(step, h0, (jnp.moveaxis(states, 1, 0), jnp.moveaxis(chunk_decay, -1, 0)))
    prev = jnp.moveaxis(prev, 0, 1)
    y_off = jnp.einsum("bclgn,bcgrpn,bgrcl->bclgrp", cc, prev, jnp.exp(a_cs))
    return (y_diag + y_off).reshape(bsz, seq, nh, hp)


def setup_inputs(seed: int = 0) -> dict:
    key = jax.random.key(seed)
    ks = jax.random.split(key, 24)
    f32 = jnp.float32
    nrm = lambda k, shp, s: jax.random.normal(k, shp, f32) * s
    dt0 = jnp.exp(jax.random.uniform(ks[8], (DEPTH, SSD_HEADS), f32, math.log(1e-3), math.log(1e-1)))
    return {
        "x": nrm(ks[0], (BATCH, SEQ, D_MODEL), 1.0),
        "c": nrm(ks[1], (BATCH, D_MODEL), 1.0),
        "ada_w": nrm(ks[2], (DEPTH, D_MODEL, N_MOD * D_MODEL), 0.5 * D_MODEL ** -0.5),
        "ada_b": nrm(ks[3], (DEPTH, N_MOD * D_MODEL), 0.02),
        "norm1_w": 1.0 + nrm(ks[4], (DEPTH, D_MODEL), 0.02),
        "w_in": nrm(ks[5], (DEPTH, D_MODEL, D_IN), D_MODEL ** -0.5),
        "ssd_conv_w": nrm(ks[6], (DEPTH, SSD_CONV, D_XBC), SSD_CONV ** -0.5),
        "ssd_conv_b": nrm(ks[7], (DEPTH, D_XBC), 0.02),
        "dt_bias": dt0 + jnp.log(-jnp.expm1(-dt0)),
        "a_log": jnp.log(jax.random.uniform(ks[9], (DEPTH, SSD_HEADS), f32, 1.0, 16.0)),
        "d_skip": 1.0 + nrm(ks[10], (DEPTH, SSD_HEADS), 0.02),
        "ssd_norm_w": 1.0 + nrm(ks[11], (DEPTH, D_SSD), 0.02),
        "conf_conv_w": nrm(ks[12], (DEPTH, CONF_CONV, D_CONF), CONF_CONV ** -0.5),
        "conf_conv_b": nrm(ks[13], (DEPTH, D_CONF), 0.02),
        "conf_ln_w": 1.0 + nrm(ks[14], (DEPTH, D_CONF), 0.02),
        "conf_ln_b": nrm(ks[15], (DEPTH, D_CONF), 0.02),
        "w_out": nrm(ks[16], (DEPTH, D_MIX, D_MODEL), D_MIX ** -0.5),
        "norm2_w": 1.0 + nrm(ks[17], (DEPTH, D_MODEL), 0.02),
        "w_up": nrm(ks[18], (DEPTH, D_MODEL, 2 * D_FF), D_MODEL ** -0.5),
        "ffn_conv_w": nrm(ks[19], (DEPTH, FFN_CONV, 2 * D_FF), FFN_CONV ** -0.5),
        "ffn_conv_b": nrm(ks[20], (DEPTH, 2 * D_FF), 0.02),
        "w_down": nrm(ks[21], (DEPTH, D_FF, D_MODEL), D_FF ** -0.5),
        "final_norm_w": 1.0 + nrm(ks[22], (D_MODEL,), 0.02),
    }


def reference(x, c, ada_w, ada_b, norm1_w, w_in, ssd_conv_w, ssd_conv_b, dt_bias,
              a_log, d_skip, ssd_norm_w, conf_conv_w, conf_conv_b, conf_ln_w,
              conf_ln_b, w_out, norm2_w, w_up, ffn_conv_w, ffn_conv_b, w_down,
              final_norm_w):
    bsz, seq, _ = x.shape
    c_act = jax.nn.silu(c)
    for i in range(DEPTH):
        mod = c_act @ ada_w[i] + ada_b[i]
        sh1, sc1, g1, sh2, sc2, g2 = [m[:, None, :] for m in jnp.split(mod, N_MOD, axis=-1)]

        h = rmsnorm(x, norm1_w[i]) * (1.0 + sc1) + sh1
        proj = h @ w_in[i]
        z = proj[..., :IN_Z]
        xbc = proj[..., IN_Z:IN_XBC]
        dt_raw = proj[..., IN_XBC:IN_DT]
        conf = proj[..., IN_DT:]

        xbc = jax.nn.silu(causal_dwconv(xbc, ssd_conv_w[i], ssd_conv_b[i]))
        xs = xbc[..., :D_SSD].reshape(bsz, seq, SSD_HEADS, SSD_HEAD_DIM)
        bs = xbc[..., D_SSD:D_SSD + SSD_GROUPS * SSD_STATE].reshape(bsz, seq, SSD_GROUPS, SSD_STATE)
        cs = xbc[..., D_SSD + SSD_GROUPS * SSD_STATE:].reshape(bsz, seq, SSD_GROUPS, SSD_STATE)
        dt = jax.nn.softplus(dt_raw.astype(jnp.float32) + dt_bias[i].astype(jnp.float32))
        a = -jnp.exp(a_log[i].astype(jnp.float32))
        y = ssd_scan(xs, dt, a, bs, cs)
        y = y + xs.astype(jnp.float32) * d_skip[i].astype(jnp.float32)[:, None]
        y = y.reshape(bsz, seq, D_SSD).astype(x.dtype)
        y_ssd = rmsnorm(y * jax.nn.silu(z), ssd_norm_w[i])

        u = conf[..., :D_CONF] * jax.nn.sigmoid(conf[..., D_CONF:])
        u = causal_dwconv(u, conf_conv_w[i], conf_conv_b[i])
        u = jax.nn.silu(layernorm(u, conf_ln_w[i], conf_ln_b[i]))

        mix = jnp.concatenate([y_ssd, u], axis=-1) @ w_out[i]
        x = x + g1 * mix

        h = rmsnorm(x, norm2_w[i]) * (1.0 + sc2) + sh2
        up = causal_dwconv(h @ w_up[i], ffn_conv_w[i], ffn_conv_b[i])
        gate, val = up[..., :D_FF], up[..., D_FF:]
        x = x + g2 * ((jax.nn.silu(gate) * val) @ w_down[i])
    return rmsnorm(x, final_norm_w)
```

```python
import functools

import jax
import jax.numpy as jnp
from jax import lax
from jax.experimental import pallas as pl
from jax.experimental.pallas import tpu as pltpu

F32 = jnp.float32
BF16 = jnp.bfloat16

LANES = 128
SUBLANES = 8
VMEM_LIMIT_BYTES = 56 * 1024 * 1024

D_MODEL = 1024
D_SSD = 1024
D_CONF = 1024
HEAD_DIM = 64
N_HEADS = D_SSD // HEAD_DIM
N_GROUPS = 2
HEADS_PER_GROUP = N_HEADS // N_GROUPS
D_STATE = 128
GROUP_WIDTH = HEADS_PER_GROUP * HEAD_DIM
D_BC = N_GROUPS * D_STATE
D_XBC = D_SSD + 2 * D_BC
SSD_CONV = 4
CONF_CONV = 31
FFN_CONV = 3
D_FF = 2816
N_MOD = 6

SEQ_TILE = 256
SSD_HALO = SUBLANES
CONF_HALO = 32
FFN_HALO = SUBLANES

COL_Z = 0
COL_XBC = COL_Z + D_SSD
COL_DT = COL_XBC + D_XBC
COL_CA = COL_DT + LANES
COL_CB = COL_CA + D_CONF
D_IN_PACKED = COL_CB + D_CONF

ROW_BLOCK = 64


def _dot(a, b):
    return jnp.dot(a, b, preferred_element_type=F32)


def _split_dot(lhs_bf16, rhs_f32, terms):
    acc = None
    rem = rhs_f32
    for i in range(terms):
        piece = rem.astype(BF16)
        part = _dot(lhs_bf16, piece)
        acc = part if acc is None else acc + part
        if i + 1 < terms:
            rem = rem - piece.astype(F32)
    return acc


def _split_dot_rhs01(lhs_f32, rhs_bf16, terms):
    acc = None
    rem = lhs_f32
    for i in range(terms):
        piece = rem.astype(BF16)
        part = _dot(piece, rhs_bf16)
        acc = part if acc is None else acc + part
        if i + 1 < terms:
            rem = rem - piece.astype(F32)
    return acc


def _silu(v):
    return v * jax.nn.sigmoid(v)


def _adaln_kernel(c_ref, w_ref, b_ref, o_ref):
    c = c_ref[...]
    c_act = _silu(c).astype(BF16)
    o_ref[...] = _dot(c_act, w_ref[...].astype(BF16)) + b_ref[...]


def _adaln(c, ada_w, ada_b):
    bsz = c.shape[0]
    n_out = ada_w.shape[1]
    blk = D_MODEL
    return pl.pallas_call(
        _adaln_kernel,
        grid=(n_out // blk,),
        in_specs=[
            pl.BlockSpec((bsz, D_MODEL), lambda j: (0, 0)),
            pl.BlockSpec((D_MODEL, blk), lambda j: (0, j)),
            pl.BlockSpec((1, blk), lambda j: (0, j)),
        ],
        out_specs=pl.BlockSpec((bsz, blk), lambda j: (0, j)),
        out_shape=jax.ShapeDtypeStruct((bsz, n_out), F32),
        compiler_params=pltpu.CompilerParams(dimension_semantics=("arbitrary",)),
        name="adaln",
    )(c, ada_w, ada_b.reshape(1, n_out))


def _mixer_kernel(x_ref, mod_ref, n1w_ref, win_ref, scw_ref, scb_ref, dtb_ref, alog_ref,
                  dskip_ref, snw_ref, ccw_ref, ccb_ref, lnw_ref, lnb_ref, wout_ref,
                  expand_ref, tri_ref,
                  o_ref,
                  xbc_ext, u_ext, uc_ref, y_ref, state_ref):
    T = SEQ_TILE

    @pl.when(pl.program_id(1) == 0)
    def _():
        xbc_ext[0:SSD_HALO, :] = jnp.zeros((SSD_HALO, D_XBC), F32)
        u_ext[0:CONF_HALO, :] = jnp.zeros((CONF_HALO, D_CONF), F32)
        state_ref[...] = jnp.zeros(state_ref.shape, F32)

    x = x_ref[...]
    ms = jnp.mean(x * x, axis=-1, keepdims=True)
    h = x * lax.rsqrt(ms + 1e-6) * n1w_ref[...]
    h = h * (1.0 + mod_ref[1:2, :]) + mod_ref[0:1, :]
    hb = h.astype(BF16)

    z = _dot(hb, win_ref[:, COL_Z:COL_Z + D_SSD])
    xbc_ext[SSD_HALO:SSD_HALO + T, :] = _dot(hb, win_ref[:, COL_XBC:COL_XBC + D_XBC])
    dt_raw = _dot(hb, win_ref[:, COL_DT:COL_DT + LANES])
    conf_a = _dot(hb, win_ref[:, COL_CA:COL_CA + D_CONF])
    conf_b = _dot(hb, win_ref[:, COL_CB:COL_CB + D_CONF])
    u_ext[CONF_HALO:CONF_HALO + T, :] = conf_a * jax.nn.sigmoid(conf_b)

    conv = scb_ref[...]
    for k in range(SSD_CONV):
        off = SSD_HALO - (SSD_CONV - 1) + k
        conv = conv + scw_ref[k:k + 1, :] * xbc_ext[off:off + T, :]
    xbc_ext[0:SSD_HALO, :] = xbc_ext[T:T + SSD_HALO, :]
    xbc = _silu(conv)
    xs = xbc[:, 0:D_SSD]
    b_mat = xbc[:, D_SSD:D_SSD + D_BC]
    c_mat = xbc[:, D_SSD + D_BC:D_XBC]

    dt = jax.nn.softplus(dt_raw + dtb_ref[...])
    a_neg = -jnp.exp(alog_ref[...])
    a_dt = dt * a_neg
    cs = _split_dot(tri_ref[...], a_dt, 3)
    cs_last = cs[T - 1:T, :]
    e_cs = jnp.exp(cs)
    d_st = jnp.exp(cs_last - cs)
    expand = expand_ref[...]
    dt_exp = _split_dot_rhs01(dt, expand, 2)
    e_exp = _split_dot_rhs01(e_cs, expand, 2)
    ds_exp = _split_dot_rhs01(d_st, expand, 2)
    xdt = xs * dt_exp
    xds_b = (xdt * ds_exp).astype(BF16)
    cs_t = cs.T

    row_i = lax.broadcasted_iota(jnp.int32, (T, T), 0)
    col_i = lax.broadcasted_iota(jnp.int32, (T, T), 1)
    causal = row_i >= col_i
    lane_i = lax.broadcasted_iota(jnp.int32, (T, LANES), 1)

    for g in range(N_GROUPS):
        bg = b_mat[:, g * D_STATE:(g + 1) * D_STATE]
        cgb = c_mat[:, g * D_STATE:(g + 1) * D_STATE].astype(BF16)
        cb = lax.dot_general(cgb, bg.astype(BF16), (((1,), (1,)), ((), ())),
                             preferred_element_type=F32)
        prev = state_ref[g]
        y_off = _dot(cgb, prev.astype(BF16))
        gcols = slice(g * GROUP_WIDTH, (g + 1) * GROUP_WIDTH)
        st_new = _dot(bg.T.astype(BF16), xds_b[:, gcols])
        state_ref[g] = prev * e_exp[T - 1:T, gcols] + st_new
        for pair in range(HEADS_PER_GROUP // 2):
            col0 = g * GROUP_WIDTH + pair * LANES
            xdt_pair = xdt[:, col0:col0 + LANES]
            acc = y_off[:, pair * LANES:(pair + 1) * LANES] * e_exp[:, col0:col0 + LANES]
            for hh in range(2):
                hd = g * HEADS_PER_GROUP + pair * 2 + hh
                diff = cs[:, hd:hd + 1] - cs_t[hd:hd + 1, :]
                decay = jnp.exp(jnp.where(causal, diff, -jnp.inf))
                m = (cb * decay).astype(BF16)
                in_half = (lane_i >= hh * HEAD_DIM) & (lane_i < (hh + 1) * HEAD_DIM)
                rhs = jnp.where(in_half, xdt_pair, 0.0).astype(BF16)
                acc = acc + _dot(m, rhs)
            y_ref[:, col0:col0 + LANES] = acc

    y = y_ref[...] + xs * dskip_ref[...]
    gated = y * _silu(z)
    ms2 = jnp.mean(gated * gated, axis=-1, keepdims=True)
    y_ssd = gated * lax.rsqrt(ms2 + 1e-6) * snw_ref[...]

    base = CONF_HALO - (CONF_CONV - 1)
    for cblk in range(D_CONF // LANES):
        cols = slice(cblk * LANES, (cblk + 1) * LANES)
        for rblk in range(T // ROW_BLOCK):
            r0 = rblk * ROW_BLOCK
            acc = jnp.broadcast_to(ccb_ref[:, cols], (ROW_BLOCK, LANES))
            for k in range(CONF_CONV):
                acc = acc + ccw_ref[k:k + 1, cols] * u_ext[r0 + base + k:r0 + base + k + ROW_BLOCK, cols]
            uc_ref[r0:r0 + ROW_BLOCK, cols] = acc
    u_ext[0:CONF_HALO, :] = u_ext[T:T + CONF_HALO, :]
    uc = uc_ref[...]
    mu = jnp.mean(uc, axis=-1, keepdims=True)
    ucc = uc - mu
    var = jnp.mean(ucc * ucc, axis=-1, keepdims=True)
    u = _silu(ucc * lax.rsqrt(var + 1e-5) * lnw_ref[...] + lnb_ref[...])

    mix = _dot(y_ssd.astype(BF16), wout_ref[0:D_SSD, :]) + _dot(u.astype(BF16), wout_ref[D_SSD:D_SSD + D_CONF, :])
    o_ref[...] = x + mod_ref[2:3, :] * mix


def _const_spec(shape):
    return pl.BlockSpec(shape, lambda b, t: (0,) * len(shape), pipeline_mode=pl.Buffered(1))


def _mixer(x, mod, n1w, win, scw, scb, dtb, alog, dskip, snw, ccw, ccb, lnw, lnb, wout, expand, tri):
    bsz, seq, _ = x.shape
    T = SEQ_TILE
    consts = (n1w, win, scw, scb, dtb, alog, dskip, snw, ccw, ccb, lnw, lnb, wout, expand, tri)
    return pl.pallas_call(
        _mixer_kernel,
        grid=(bsz, seq // T),
        in_specs=[
            pl.BlockSpec((None, T, D_MODEL), lambda b, t: (b, t, 0)),
            pl.BlockSpec((None, N_MOD, D_MODEL), lambda b, t: (b, 0, 0)),
        ] + [_const_spec(a.shape) for a in consts],
        out_specs=pl.BlockSpec((None, T, D_MODEL), lambda b, t: (b, t, 0)),
        out_shape=jax.ShapeDtypeStruct(x.shape, F32),
        scratch_shapes=[
            pltpu.VMEM((SSD_HALO + T, D_XBC), F32),
            pltpu.VMEM((CONF_HALO + T, D_CONF), F32),
            pltpu.VMEM((T, D_CONF), F32),
            pltpu.VMEM((T, D_SSD), F32),
            pltpu.VMEM((N_GROUPS, D_STATE, GROUP_WIDTH), F32),
        ],
        compiler_params=pltpu.CompilerParams(
            dimension_semantics=("arbitrary", "arbitrary"),
            vmem_limit_bytes=VMEM_LIMIT_BYTES),
        name="mixer",
    )(x, mod, *consts)


FFN_COL_BLOCK = 256


def _ffn_kernel(apply_final_norm, x_ref, mod_ref, n2w_ref, wup_ref, fcw_ref, fcb_ref, wdown_ref,
                fnw_ref, o_ref, up_ext, act_ref):
    T = SEQ_TILE

    @pl.when(pl.program_id(1) == 0)
    def _():
        up_ext[0:FFN_HALO, :] = jnp.zeros((FFN_HALO, 2 * D_FF), F32)

    x = x_ref[...]
    ms = jnp.mean(x * x, axis=-1, keepdims=True)
    h = x * lax.rsqrt(ms + 1e-6) * n2w_ref[...]
    h = h * (1.0 + mod_ref[4:5, :]) + mod_ref[3:4, :]
    up_ext[FFN_HALO:FFN_HALO + T, :] = _dot(h.astype(BF16), wup_ref[...])

    base = FFN_HALO - (FFN_CONV - 1)

    def conv_cols(r0, cols):
        acc = jnp.broadcast_to(fcb_ref[:, cols], (ROW_BLOCK, cols.stop - cols.start))
        for k in range(FFN_CONV):
            acc = acc + fcw_ref[k:k + 1, cols] * up_ext[r0 + base + k:r0 + base + k + ROW_BLOCK, cols]
        return acc

    for cblk in range(D_FF // FFN_COL_BLOCK):
        gcols = slice(cblk * FFN_COL_BLOCK, (cblk + 1) * FFN_COL_BLOCK)
        vcols = slice(D_FF + cblk * FFN_COL_BLOCK, D_FF + (cblk + 1) * FFN_COL_BLOCK)
        for rblk in range(T // ROW_BLOCK):
            r0 = rblk * ROW_BLOCK
            gate = conv_cols(r0, gcols)
            val = conv_cols(r0, vcols)
            act_ref[r0:r0 + ROW_BLOCK, gcols] = (_silu(gate) * val).astype(BF16)
    up_ext[0:FFN_HALO, :] = up_ext[T:T + FFN_HALO, :]

    down = _dot(act_ref[...], wdown_ref[...])
    x2 = x + mod_ref[5:6, :] * down
    if apply_final_norm:
        ms2 = jnp.mean(x2 * x2, axis=-1, keepdims=True)
        x2 = x2 * lax.rsqrt(ms2 + 1e-6) * fnw_ref[...]
    o_ref[...] = x2


def _ffn(x, mod, n2w, wup, fcw, fcb, wdown, fnw, apply_final_norm):
    bsz, seq, _ = x.shape
    T = SEQ_TILE
    consts = (n2w, wup, fcw, fcb, wdown, fnw)
    return pl.pallas_call(
        functools.partial(_ffn_kernel, apply_final_norm),
        grid=(bsz, seq // T),
        in_specs=[
            pl.BlockSpec((None, T, D_MODEL), lambda b, t: (b, t, 0)),
            pl.BlockSpec((None, N_MOD, D_MODEL), lambda b, t: (b, 0, 0)),
        ] + [_const_spec(a.shape) for a in consts],
        out_specs=pl.BlockSpec((None, T, D_MODEL), lambda b, t: (b, t, 0)),
        out_shape=jax.ShapeDtypeStruct(x.shape, F32),
        scratch_shapes=[
            pltpu.VMEM((FFN_HALO + T, 2 * D_FF), F32),
            pltpu.VMEM((T, D_FF), BF16),
        ],
        compiler_params=pltpu.CompilerParams(
            dimension_semantics=("arbitrary", "arbitrary"),
            vmem_limit_bytes=VMEM_LIMIT_BYTES),
        name="ffn",
    )(x, mod, *consts)


def _pad_lanes(v, width):
    return jnp.pad(v, ((0, 0), (0, width - v.shape[1])))


def _pack_w_in(w_in):
    in_z = D_SSD
    in_xbc = in_z + D_XBC
    in_dt = in_xbc + N_HEADS
    return jnp.concatenate([
        w_in[:, :in_z],
        w_in[:, in_z:in_xbc],
        _pad_lanes(w_in[:, in_xbc:in_dt], LANES),
        w_in[:, in_dt:in_dt + D_CONF],
        w_in[:, in_dt + D_CONF:],
    ], axis=1).astype(BF16)


def kernel(x, c, ada_w, ada_b, norm1_w, w_in, ssd_conv_w, ssd_conv_b, dt_bias, a_log, d_skip,
           ssd_norm_w, conf_conv_w, conf_conv_b, conf_ln_w, conf_ln_b, w_out, norm2_w, w_up,
           ffn_conv_w, ffn_conv_b, w_down, final_norm_w):
    depth = ada_w.shape[0]
    bsz = x.shape[0]
    row = lambda v: v.reshape(1, -1)
    head_of_lane = jnp.arange(D_SSD, dtype=jnp.int32) // HEAD_DIM
    expand = (jnp.arange(LANES, dtype=jnp.int32)[:, None] == head_of_lane[None, :]).astype(BF16)
    tok = jnp.arange(SEQ_TILE, dtype=jnp.int32)
    tri = (tok[:, None] >= tok[None, :]).astype(BF16)
    for i in range(depth):
        mod = _adaln(c, ada_w[i], ada_b[i]).reshape(bsz, N_MOD, D_MODEL)
        x = _mixer(
            x, mod, row(norm1_w[i]), _pack_w_in(w_in[i]),
            ssd_conv_w[i], row(ssd_conv_b[i]),
            _pad_lanes(row(dt_bias[i]), LANES), _pad_lanes(row(a_log[i]), LANES),
            row(jnp.repeat(d_skip[i], HEAD_DIM)), row(ssd_norm_w[i]),
            conf_conv_w[i], row(conf_conv_b[i]), row(conf_ln_w[i]), row(conf_ln_b[i]),
            w_out[i].astype(BF16), expand, tri)
        x = _ffn(
            x, mod, row(norm2_w[i]), w_up[i].astype(BF16), ffn_conv_w[i], row(ffn_conv_b[i]),
            w_down[i].astype(BF16), row(final_norm_w), apply_final_norm=(i == depth - 1))
    return x
```

```python
import functools

import jax
import jax.numpy as jnp
from jax import lax
from jax.experimental import pallas as pl
from jax.experimental.pallas import tpu as pltpu

F32 = jnp.float32
BF16 = jnp.bfloat16

LANES = 128
SUBLANES = 8
VMEM_LIMIT_BYTES = 56 * 1024 * 1024

D_MODEL = 1024
D_SSD = 1024
D_CONF = 1024
HEAD_DIM = 64
N_HEADS = D_SSD // HEAD_DIM
N_GROUPS = 2
HEADS_PER_GROUP = N_HEADS // N_GROUPS
D_STATE = 128
GROUP_WIDTH = HEADS_PER_GROUP * HEAD_DIM
D_BC = N_GROUPS * D_STATE
D_XBC = D_SSD + 2 * D_BC
SSD_CONV = 4
CONF_CONV = 31
FFN_CONV = 3
D_FF = 2816
N_MOD = 6

SEQ_TILE = 256
SSD_HALO = SUBLANES
CONF_HALO = 32
FFN_HALO = SUBLANES

XBC_PITCH = D_XBC // LANES + 1
CONF_PITCH = D_CONF // LANES + 1
FFN_PITCH = 2 * D_FF // LANES + 1

COL_Z = 0
COL_XBC = COL_Z + D_SSD
COL_DT = COL_XBC + D_XBC
COL_CA = COL_DT + LANES
COL_CB = COL_CA + D_CONF
D_IN_PACKED = COL_CB + D_CONF

ROW_BLOCK = 64


def _dot(a, b):
    return jnp.dot(a, b, preferred_element_type=F32)


def _split_dot(lhs_bf16, rhs_f32, terms):
    acc = None
    rem = rhs_f32
    for i in range(terms):
        piece = rem.astype(BF16)
        part = _dot(lhs_bf16, piece)
        acc = part if acc is None else acc + part
        if i + 1 < terms:
            rem = rem - piece.astype(F32)
    return acc


def _split_dot_rhs01(lhs_f32, rhs_bf16, terms):
    acc = None
    rem = lhs_f32
    for i in range(terms):
        piece = rem.astype(BF16)
        part = _dot(piece, rhs_bf16)
        acc = part if acc is None else acc + part
        if i + 1 < terms:
            rem = rem - piece.astype(F32)
    return acc


def _silu(v):
    return v * jax.nn.sigmoid(v)


def _slab_rows(token0, n_tokens, pitch, tile):
    return pl.ds(token0 * pitch + tile, n_tokens, stride=pitch)


def _slab_store(slab_ref, token0, pitch, value):
    for j in range(value.shape[1] // LANES):
        slab_ref[_slab_rows(token0, value.shape[0], pitch, j), :] = value[:, j * LANES:(j + 1) * LANES]


def _slab_keep_tail(slab_ref, n_tokens, halo, pitch):
    slab_ref[0:halo * pitch, :] = slab_ref[n_tokens * pitch:(n_tokens + halo) * pitch, :]


def _slab_conv(slab_ref, w_ref, b_ref, halo, pitch, tile, token0, n_tokens):
    taps = w_ref.shape[0]
    cols = slice(tile * LANES, (tile + 1) * LANES)
    acc = jnp.broadcast_to(b_ref[:, cols], (n_tokens, LANES))
    for k in range(taps):
        first = halo + token0 - (taps - 1) + k
        acc = acc + w_ref[k:k + 1, cols] * slab_ref[_slab_rows(first, n_tokens, pitch, tile), :]
    return acc


def _adaln_kernel(c_ref, w_ref, b_ref, o_ref):
    c = c_ref[...]
    c_act = _silu(c).astype(BF16)
    o_ref[...] = _dot(c_act, w_ref[...].astype(BF16)) + b_ref[...]


def _adaln(c, ada_w, ada_b):
    bsz = c.shape[0]
    n_out = ada_w.shape[1]
    blk = D_MODEL
    return pl.pallas_call(
        _adaln_kernel,
        grid=(n_out // blk,),
        in_specs=[
            pl.BlockSpec((bsz, D_MODEL), lambda j: (0, 0)),
            pl.BlockSpec((D_MODEL, blk), lambda j: (0, j)),
            pl.BlockSpec((1, blk), lambda j: (0, j)),
        ],
        out_specs=pl.BlockSpec((bsz, blk), lambda j: (0, j)),
        out_shape=jax.ShapeDtypeStruct((bsz, n_out), F32),
        compiler_params=pltpu.CompilerParams(dimension_semantics=("arbitrary",)),
        name="adaln",
    )(c, ada_w, ada_b.reshape(1, n_out))


def _mixer_kernel(x_ref, mod_ref, n1w_ref, win_ref, scw_ref, scb_ref, dtb_ref, alog_ref,
                  dskip_ref, snw_ref, ccw_ref, ccb_ref, lnw_ref, lnb_ref, wout_ref,
                  expand_ref, tri_ref,
                  o_ref,
                  xbc_slab, u_slab, xbc_ref, uc_ref, y_ref, state_ref):
    T = SEQ_TILE

    @pl.when(pl.program_id(1) == 0)
    def _():
        xbc_slab[0:SSD_HALO * XBC_PITCH, :] = jnp.zeros((SSD_HALO * XBC_PITCH, LANES), F32)
        u_slab[0:CONF_HALO * CONF_PITCH, :] = jnp.zeros((CONF_HALO * CONF_PITCH, LANES), F32)
        state_ref[...] = jnp.zeros(state_ref.shape, F32)

    x = x_ref[...]
    ms = jnp.mean(x * x, axis=-1, keepdims=True)
    h = x * lax.rsqrt(ms + 1e-6) * n1w_ref[...]
    h = h * (1.0 + mod_ref[1:2, :]) + mod_ref[0:1, :]
    hb = h.astype(BF16)

    z = _dot(hb, win_ref[:, COL_Z:COL_Z + D_SSD])
    _slab_store(xbc_slab, SSD_HALO, XBC_PITCH, _dot(hb, win_ref[:, COL_XBC:COL_XBC + D_XBC]))
    dt_raw = _dot(hb, win_ref[:, COL_DT:COL_DT + LANES])
    conf_a = _dot(hb, win_ref[:, COL_CA:COL_CA + D_CONF])
    conf_b = _dot(hb, win_ref[:, COL_CB:COL_CB + D_CONF])
    _slab_store(u_slab, CONF_HALO, CONF_PITCH, conf_a * jax.nn.sigmoid(conf_b))

    for tile in range(D_XBC // LANES):
        for rblk in range(T // ROW_BLOCK):
            r0 = rblk * ROW_BLOCK
            conv = _slab_conv(xbc_slab, scw_ref, scb_ref, SSD_HALO, XBC_PITCH, tile, r0, ROW_BLOCK)
            xbc_ref[r0:r0 + ROW_BLOCK, tile * LANES:(tile + 1) * LANES] = _silu(conv)
    _slab_keep_tail(xbc_slab, T, SSD_HALO, XBC_PITCH)
    xs = xbc_ref[:, 0:D_SSD]
    b_mat = xbc_ref[:, D_SSD:D_SSD + D_BC]
    c_mat = xbc_ref[:, D_SSD + D_BC:D_XBC]

    dt = jax.nn.softplus(dt_raw + dtb_ref[...])
    a_neg = -jnp.exp(alog_ref[...])
    a_dt = dt * a_neg
    cs = _split_dot(tri_ref[...], a_dt, 3)
    cs_last = cs[T - 1:T, :]
    e_cs = jnp.exp(cs)
    d_st = jnp.exp(cs_last - cs)
    expand = expand_ref[...]
    dt_exp = _split_dot_rhs01(dt, expand, 2)
    e_exp = _split_dot_rhs01(e_cs, expand, 2)
    ds_exp = _split_dot_rhs01(d_st, expand, 2)
    xdt = xs * dt_exp
    xds_b = (xdt * ds_exp).astype(BF16)
    cs_t = cs.T

    row_i = lax.broadcasted_iota(jnp.int32, (T, T), 0)
    col_i = lax.broadcasted_iota(jnp.int32, (T, T), 1)
    causal = row_i >= col_i
    lane_i = lax.broadcasted_iota(jnp.int32, (T, LANES), 1)

    for g in range(N_GROUPS):
        bg = b_mat[:, g * D_STATE:(g + 1) * D_STATE]
        cgb = c_mat[:, g * D_STATE:(g + 1) * D_STATE].astype(BF16)
        cb = lax.dot_general(cgb, bg.astype(BF16), (((1,), (1,)), ((), ())),
                             preferred_element_type=F32)
        prev = state_ref[g]
        y_off = _dot(cgb, prev.astype(BF16))
        gcols = slice(g * GROUP_WIDTH, (g + 1) * GROUP_WIDTH)
        st_new = _dot(bg.T.astype(BF16), xds_b[:, gcols])
        state_ref[g] = prev * e_exp[T - 1:T, gcols] + st_new
        for pair in range(HEADS_PER_GROUP // 2):
            col0 = g * GROUP_WIDTH + pair * LANES
            xdt_pair = xdt[:, col0:col0 + LANES]
            acc = y_off[:, pair * LANES:(pair + 1) * LANES] * e_exp[:, col0:col0 + LANES]
            for hh in range(2):
                hd = g * HEADS_PER_GROUP + pair * 2 + hh
                diff = cs[:, hd:hd + 1] - cs_t[hd:hd + 1, :]
                decay = jnp.exp(jnp.where(causal, diff, -jnp.inf))
                m = (cb * decay).astype(BF16)
                in_half = (lane_i >= hh * HEAD_DIM) & (lane_i < (hh + 1) * HEAD_DIM)
                rhs = jnp.where(in_half, xdt_pair, 0.0).astype(BF16)
                acc = acc + _dot(m, rhs)
            y_ref[:, col0:col0 + LANES] = acc

    y = y_ref[...] + xs * dskip_ref[...]
    gated = y * _silu(z)
    ms2 = jnp.mean(gated * gated, axis=-1, keepdims=True)
    y_ssd = gated * lax.rsqrt(ms2 + 1e-6) * snw_ref[...]

    for tile in range(D_CONF // LANES):
        for rblk in range(T // ROW_BLOCK):
            r0 = rblk * ROW_BLOCK
            uc_ref[r0:r0 + ROW_BLOCK, tile * LANES:(tile + 1) * LANES] = _slab_conv(
                u_slab, ccw_ref, ccb_ref, CONF_HALO, CONF_PITCH, tile, r0, ROW_BLOCK)
    _slab_keep_tail(u_slab, T, CONF_HALO, CONF_PITCH)
    uc = uc_ref[...]
    mu = jnp.mean(uc, axis=-1, keepdims=True)
    ucc = uc - mu
    var = jnp.mean(ucc * ucc, axis=-1, keepdims=True)
    u = _silu(ucc * lax.rsqrt(var + 1e-5) * lnw_ref[...] + lnb_ref[...])

    mix = _dot(y_ssd.astype(BF16), wout_ref[0:D_SSD, :]) + _dot(u.astype(BF16), wout_ref[D_SSD:D_SSD + D_CONF, :])
    o_ref[...] = x + mod_ref[2:3, :] * mix


def _const_spec(shape):
    return pl.BlockSpec(shape, lambda b, t: (0,) * len(shape), pipeline_mode=pl.Buffered(1))


def _mixer(x, mod, n1w, win, scw, scb, dtb, alog, dskip, snw, ccw, ccb, lnw, lnb, wout, expand, tri):
    bsz, seq, _ = x.shape
    T = SEQ_TILE
    consts = (n1w, win, scw, scb, dtb, alog, dskip, snw, ccw, ccb, lnw, lnb, wout, expand, tri)
    return pl.pallas_call(
        _mixer_kernel,
        grid=(bsz, seq // T),
        in_specs=[
            pl.BlockSpec((None, T, D_MODEL), lambda b, t: (b, t, 0)),
            pl.BlockSpec((None, N_MOD, D_MODEL), lambda b, t: (b, 0, 0)),
        ] + [_const_spec(a.shape) for a in consts],
        out_specs=pl.BlockSpec((None, T, D_MODEL), lambda b, t: (b, t, 0)),
        out_shape=jax.ShapeDtypeStruct(x.shape, F32),
        scratch_shapes=[
            pltpu.VMEM(((SSD_HALO + T) * XBC_PITCH, LANES), F32),
            pltpu.VMEM(((CONF_HALO + T) * CONF_PITCH, LANES), F32),
            pltpu.VMEM((T, D_XBC), F32),
            pltpu.VMEM((T, D_CONF), F32),
            pltpu.VMEM((T, D_SSD), F32),
            pltpu.VMEM((N_GROUPS, D_STATE, GROUP_WIDTH), F32),
        ],
        compiler_params=pltpu.CompilerParams(
            dimension_semantics=("arbitrary", "arbitrary"),
            vmem_limit_bytes=VMEM_LIMIT_BYTES),
        name="mixer",
    )(x, mod, *consts)


def _ffn_kernel(apply_final_norm, x_ref, mod_ref, n2w_ref, wup_ref, fcw_ref, fcb_ref, wdown_ref,
                fnw_ref, o_ref, up_slab, act_ref):
    T = SEQ_TILE

    @pl.when(pl.program_id(1) == 0)
    def _():
        up_slab[0:FFN_HALO * FFN_PITCH, :] = jnp.zeros((FFN_HALO * FFN_PITCH, LANES), F32)

    x = x_ref[...]
    ms = jnp.mean(x * x, axis=-1, keepdims=True)
    h = x * lax.rsqrt(ms + 1e-6) * n2w_ref[...]
    h = h * (1.0 + mod_ref[4:5, :]) + mod_ref[3:4, :]
    _slab_store(up_slab, FFN_HALO, FFN_PITCH, _dot(h.astype(BF16), wup_ref[...]))

    gate_tiles = D_FF // LANES
    for tile in range(gate_tiles):
        for rblk in range(T // ROW_BLOCK):
            r0 = rblk * ROW_BLOCK
            gate = _slab_conv(up_slab, fcw_ref, fcb_ref, FFN_HALO, FFN_PITCH, tile, r0, ROW_BLOCK)
            val = _slab_conv(up_slab, fcw_ref, fcb_ref, FFN_HALO, FFN_PITCH, gate_tiles + tile, r0, ROW_BLOCK)
            act_ref[r0:r0 + ROW_BLOCK, tile * LANES:(tile + 1) * LANES] = (_silu(gate) * val).astype(BF16)
    _slab_keep_tail(up_slab, T, FFN_HALO, FFN_PITCH)

    down = _dot(act_ref[...], wdown_ref[...])
    x2 = x + mod_ref[5:6, :] * down
    if apply_final_norm:
        ms2 = jnp.mean(x2 * x2, axis=-1, keepdims=True)
        x2 = x2 * lax.rsqrt(ms2 + 1e-6) * fnw_ref[...]
    o_ref[...] = x2


def _ffn(x, mod, n2w, wup, fcw, fcb, wdown, fnw, apply_final_norm):
    bsz, seq, _ = x.shape
    T = SEQ_TILE
    consts = (n2w, wup, fcw, fcb, wdown, fnw)
    return pl.pallas_call(
        functools.partial(_ffn_kernel, apply_final_norm),
        grid=(bsz, seq // T),
        in_specs=[
            pl.BlockSpec((None, T, D_MODEL), lambda b, t: (b, t, 0)),
            pl.BlockSpec((None, N_MOD, D_MODEL), lambda b, t: (b, 0, 0)),
        ] + [_const_spec(a.shape) for a in consts],
        out_specs=pl.BlockSpec((None, T, D_MODEL), lambda b, t: (b, t, 0)),
        out_shape=jax.ShapeDtypeStruct(x.shape, F32),
        scratch_shapes=[
            pltpu.VMEM(((FFN_HALO + T) * FFN_PITCH, LANES), F32),
            pltpu.VMEM((T, D_FF), BF16),
        ],
        compiler_params=pltpu.CompilerParams(
            dimension_semantics=("arbitrary", "arbitrary"),
            vmem_limit_bytes=VMEM_LIMIT_BYTES),
        name="ffn",
    )(x, mod, *consts)


def _pad_lanes(v, width):
    return jnp.pad(v, ((0, 0), (0, width - v.shape[1])))


def _pack_w_in(w_in):
    in_z = D_SSD
    in_xbc = in_z + D_XBC
    in_dt = in_xbc + N_HEADS
    return jnp.concatenate([
        w_in[:, :in_z],
        w_in[:, in_z:in_xbc],
        _pad_lanes(w_in[:, in_xbc:in_dt], LANES),
        w_in[:, in_dt:in_dt + D_CONF],
        w_in[:, in_dt + D_CONF:],
    ], axis=1).astype(BF16)


def kernel(x, c, ada_w, ada_b, norm1_w, w_in, ssd_conv_w, ssd_conv_b, dt_bias, a_log, d_skip,
           ssd_norm_w, conf_conv_w, conf_conv_b, conf_ln_w, conf_ln_b, w_out, norm2_w, w_up,
           ffn_conv_w, ffn_conv_b, w_down, final_norm_w):
    depth = ada_w.shape[0]
    bsz = x.shape[0]
    row = lambda v: v.reshape(1, -1)
    head_of_lane = jnp.arange(D_SSD, dtype=jnp.int32) // HEAD_DIM
    expand = (jnp.arange(LANES, dtype=jnp.int32)[:, None] == head_of_lane[None, :]).astype(BF16)
    tok = jnp.arange(SEQ_TILE, dtype=jnp.int32)
    tri = (tok[:, None] >= tok[None, :]).astype(BF16)
    for i in range(depth):
        mod = _adaln(c, ada_w[i], ada_b[i]).reshape(bsz, N_MOD, D_MODEL)
        x = _mixer(
            x, mod, row(norm1_w[i]), _pack_w_in(w_in[i]),
            ssd_conv_w[i], row(ssd_conv_b[i]),
            _pad_lanes(row(dt_bias[i]), LANES), _pad_lanes(row(a_log[i]), LANES),
            row(jnp.repeat(d_skip[i], HEAD_DIM)), row(ssd_norm_w[i]),
            conf_conv_w[i], row(conf_conv_b[i]), row(conf_ln_w[i]), row(conf_ln_b[i]),
            w_out[i].astype(BF16), expand, tri)
        x = _ffn(
            x, mod, row(norm2_w[i]), w_up[i].astype(BF16), ffn_conv_w[i], row(ffn_conv_b[i]),
            w_down[i].astype(BF16), row(final_norm_w), apply_final_norm=(i == depth - 1))
    return x
```
